```python
import jax, jax.numpy as jnp
from jax import lax
import numpy as np

D_MODEL = 2048
BATCH = 4
SEQ = 4096
DEPTH = 2

N_EVEN = (DEPTH + 1) // 2
N_ODD = DEPTH // 2
W_A = D_MODEL // 2
W_B = D_MODEL // 2
HEAD_DIM = 64
CONV_A = 3
CONV_B = 31
W_C = D_MODEL
POOL_WINDOWS = (2, 4, 8, 16)
N_POOL_GROUPS = len(POOL_WINDOWS)
G_C = W_C // N_POOL_GROUPS
EVEN_IN = 4 * W_A + 3 * W_B
ODD_IN = 2 * W_C
EPS = 1e-6

kernel_name = "hybrid_shortconv_conformer_pool_sandwich"


def _rmsnorm(x, g):
    xf = x.astype(jnp.float32)
    r = lax.rsqrt(jnp.mean(xf * xf, axis=-1, keepdims=True) + EPS)
    return (xf * r).astype(x.dtype) * g


def _layernorm(x, g, b):
    xf = x.astype(jnp.float32)
    mu = jnp.mean(xf, axis=-1, keepdims=True)
    var = jnp.mean(jnp.square(xf - mu), axis=-1, keepdims=True)
    return ((xf - mu) * lax.rsqrt(var + EPS)).astype(x.dtype) * g + b


def _causal_dwconv(x, w):
    k, c = w.shape
    return lax.conv_general_dilated(
        x, w[:, None, :].astype(x.dtype), window_strides=(1,),
        padding=[(k - 1, 0)], dimension_numbers=("NWC", "WIO", "NWC"),
        feature_group_count=c)


def _causal_pool_means(v):
    s = v.shape[1]
    cs = jnp.cumsum(v.astype(jnp.float32), axis=1)
    pos = jnp.arange(1, s + 1, dtype=jnp.int32)
    outs = []
    for win, c in zip(POOL_WINDOWS, jnp.split(cs, N_POOL_GROUPS, axis=-1)):
        shifted = jnp.pad(c, ((0, 0), (win, 0), (0, 0)))[:, :s]
        cnt = jnp.minimum(pos, win).astype(jnp.float32)[None, :, None]
        outs.append((c - shifted) / cnt)
    return jnp.concatenate(outs, axis=-1).astype(v.dtype)


def _even_mixer(h, w_in, a_conv, b_conv, b_conv_bias, b_ln_g, b_ln_b, w_out):
    p = h @ w_in
    a_x, a_b, a_c, a_z, b_val, b_gate, b_z = jnp.split(p, 7, axis=-1)
    ya = a_b * _causal_dwconv(a_c * a_x, a_conv)
    yb = b_val * jax.nn.sigmoid(b_gate)
    yb = _causal_dwconv(yb, b_conv) + b_conv_bias
    yb = jax.nn.silu(_layernorm(yb, b_ln_g, b_ln_b))
    u = jnp.concatenate([ya * jax.nn.silu(a_z), yb * jax.nn.silu(b_z)], axis=-1)
    return u @ w_out


def _odd_mixer(h, w_in, c_w, c_b, c_scale, w_out):
    p = h @ w_in
    v, z = jnp.split(p, 2, axis=-1)
    pooled = _causal_pool_means(v) - v
    bsz, s, _ = v.shape
    g = pooled.reshape(bsz, s, N_POOL_GROUPS, G_C)
    g = jnp.einsum("bsgc,gcd->bsgd", g, c_w) + c_b
    y = g.reshape(bsz, s, W_C) * c_scale
    return (y * jax.nn.silu(z)) @ w_out


def setup_inputs(seed: int = 0) -> dict:
    key = jax.random.key(seed)
    ks = jax.random.split(key, 24)
    f32 = jnp.float32

    def nrm(k, shape, scale):
        return jax.random.normal(k, shape, f32) * scale

    def gain(k, shape):
        return 1.0 + 0.05 * jax.random.normal(k, shape, f32)

    return {
        "x": jax.random.normal(ks[0], (BATCH, SEQ, D_MODEL), f32),
        "e_norm_pre": gain(ks[1], (N_EVEN, D_MODEL)),
        "e_norm_post": gain(ks[2], (N_EVEN, D_MODEL)),
        "e_w_in": nrm(ks[3], (N_EVEN, D_MODEL, EVEN_IN), D_MODEL ** -0.5),
        "e_a_conv": nrm(ks[4], (N_EVEN, CONV_A, W_A), CONV_A ** -0.5),
        "e_b_conv": nrm(ks[5], (N_EVEN, CONV_B, W_B), CONV_B ** -0.5),
        "e_b_conv_bias": nrm(ks[6], (N_EVEN, W_B), 0.02),
        "e_b_ln_g": gain(ks[7], (N_EVEN, W_B)),
        "e_b_ln_b": nrm(ks[8], (N_EVEN, W_B), 0.02),
        "e_w_out": nrm(ks[9], (N_EVEN, W_A + W_B, D_MODEL), (W_A + W_B) ** -0.5),
        "o_norm_pre": gain(ks[10], (N_ODD, D_MODEL)),
        "o_norm_post": gain(ks[11], (N_ODD, D_MODEL)),
        "o_w_in": nrm(ks[12], (N_ODD, D_MODEL, ODD_IN), D_MODEL ** -0.5),
        "o_c_w": nrm(ks[13], (N_ODD, N_POOL_GROUPS, G_C, G_C), G_C ** -0.5),
        "o_c_b": nrm(ks[14], (N_ODD, N_POOL_GROUPS, G_C), 0.02),
        "o_c_scale": gain(ks[15], (N_ODD, W_C)),
        "o_w_out": nrm(ks[16], (N_ODD, W_C, D_MODEL), W_C ** -0.5),
    }


def reference(x, e_norm_pre, e_norm_post, e_w_in, e_a_conv, e_b_conv, e_b_conv_bias,
              e_b_ln_g, e_b_ln_b, e_w_out, o_norm_pre, o_norm_post, o_w_in, o_c_w,
              o_c_b, o_c_scale, o_w_out):
    for layer in range(DEPTH):
        i = layer // 2
        if layer % 2 == 0:
            h = _rmsnorm(x, e_norm_pre[i])
            y = _even_mixer(h, e_w_in[i], e_a_conv[i], e_b_conv[i], e_b_conv_bias[i],
                            e_b_ln_g[i], e_b_ln_b[i], e_w_out[i])
            x = x + _rmsnorm(y, e_norm_post[i])
        else:
            h = _rmsnorm(x, o_norm_pre[i])
            y = _odd_mixer(h, o_w_in[i], o_c_w[i], o_c_b[i], o_c_scale[i], o_w_out[i])
            x = x + _rmsnorm(y, o_norm_post[i])
    return x
```

```python
import functools

import jax
import jax.numpy as jnp
from jax import lax
from jax.experimental import pallas as pl
from jax.experimental.pallas import tpu as pltpu

D_MODEL = 2048
W_BRANCH = D_MODEL // 2
CONV_A = 3
CONV_B = 31
POOL_WINDOWS = (2, 4, 8, 16)
G_C = D_MODEL // len(POOL_WINDOWS)
EPS = 1e-6

SUBLANES = 8
HALO = 32
SEQ_TILE = 256
VMEM_LIMIT_BYTES = 60 * 1024 * 1024

_F32 = jnp.float32
_BF16 = jnp.bfloat16


def _rmsnorm(xf, g):
    r = lax.rsqrt(jnp.mean(xf * xf, axis=-1, keepdims=True) + EPS)
    return xf * r * g


def _silu(v):
    return v * jax.nn.sigmoid(v)


def _even_layer_kernel(x_ref, gpre_ref, win_ref, aconv_ref, bconv_ref, bbias_ref,
                       lng_ref, lnb_ref, wout_ref, gpost_ref, o_ref,
                       mbuf, gbuf, u_ref):
    t = x_ref.shape[1]
    w = W_BRANCH

    @pl.when(pl.program_id(1) == 0)
    def _():
        mbuf[0:HALO, :] = jnp.zeros((HALO, w), _F32)
        gbuf[0:HALO, :] = jnp.zeros((HALO, w), _F32)

    x = x_ref[0]
    h = _rmsnorm(x, gpre_ref[...]).astype(_BF16)

    def proj(j):
        return jnp.dot(h, win_ref[:, j * w:(j + 1) * w], preferred_element_type=_F32)

    mbuf[HALO:HALO + t, :] = proj(2) * proj(0)
    ya = aconv_ref[0:1, :] * mbuf[HALO - 2:HALO - 2 + t, :]
    for k in range(1, CONV_A):
        ya = ya + aconv_ref[k:k + 1, :] * mbuf[HALO - 2 + k:HALO - 2 + k + t, :]
    u_ref[:, 0:w] = (proj(1) * ya * _silu(proj(3))).astype(_BF16)

    gbuf[HALO:HALO + t, :] = proj(4) * jax.nn.sigmoid(proj(5))
    base = HALO - (CONV_B - 1)
    yb = bconv_ref[0:1, :] * gbuf[base:base + t, :]
    for k in range(1, CONV_B):
        yb = yb + bconv_ref[k:k + 1, :] * gbuf[base + k:base + k + t, :]
    yb = yb + bbias_ref[...]
    mu = jnp.mean(yb, axis=-1, keepdims=True)
    d = yb - mu
    var = jnp.mean(d * d, axis=-1, keepdims=True)
    ln = d * lax.rsqrt(var + EPS) * lng_ref[...] + lnb_ref[...]
    u_ref[:, w:2 * w] = (_silu(ln) * _silu(proj(6))).astype(_BF16)

    y = jnp.dot(u_ref[...], wout_ref[...], preferred_element_type=_F32)
    o_ref[0] = x + _rmsnorm(y, gpost_ref[...])

    mbuf[0:HALO, :] = mbuf[t:t + HALO, :]
    gbuf[0:HALO, :] = gbuf[t:t + HALO, :]


def _odd_layer_kernel(x_ref, gpre_ref, win_ref, cw_ref, cb_ref, cscale_ref,
                      wout_ref, gpost_ref, o_ref, vbuf, u_ref):
    t = x_ref.shape[1]
    s = pl.program_id(1)

    @pl.when(s == 0)
    def _():
        vbuf[0:HALO, :] = jnp.zeros((HALO, D_MODEL), _F32)

    x = x_ref[0]
    h = _rmsnorm(x, gpre_ref[...]).astype(_BF16)
    vbuf[HALO:HALO + t, :] = jnp.dot(h, win_ref[:, 0:D_MODEL], preferred_element_type=_F32)
    z = jnp.dot(h, win_ref[:, D_MODEL:2 * D_MODEL], preferred_element_type=_F32)

    pos = s * t + lax.broadcasted_iota(jnp.int32, (t, G_C), 0) + 1
    for gi, win in enumerate(POOL_WINDOWS):
        lo, hi = gi * G_C, (gi + 1) * G_C
        v = vbuf[HALO:HALO + t, lo:hi]
        acc = v
        for j in range(1, win):
            acc = acc + vbuf[HALO - j:HALO - j + t, lo:hi]
        cnt = jnp.minimum(pos, win).astype(_F32)
        pooled = (acc / cnt - v).astype(_BF16)
        g = jnp.dot(pooled, cw_ref[gi], preferred_element_type=_F32) + cb_ref[:, lo:hi]
        y = g * cscale_ref[:, lo:hi]
        u_ref[:, lo:hi] = (y * _silu(z[:, lo:hi])).astype(_BF16)

    y = jnp.dot(u_ref[...], wout_ref[...], preferred_element_type=_F32)
    o_ref[0] = x + _rmsnorm(y, gpost_ref[...])

    vbuf[0:HALO, :] = vbuf[t:t + HALO, :]


def _resident(shape):
    return pl.BlockSpec(shape, lambda b, s: (0,) * len(shape), pipeline_mode=pl.Buffered(1))


def _layer_call(body, x, params, scratch_shapes, name):
    bsz, seq, d = x.shape
    t = SEQ_TILE
    assert seq % t == 0 and t >= HALO
    x_spec = pl.BlockSpec((1, t, d), lambda b, s: (b, s, 0))
    return pl.pallas_call(
        body,
        out_shape=jax.ShapeDtypeStruct(x.shape, x.dtype),
        grid=(bsz, seq // t),
        in_specs=[x_spec] + [_resident(p.shape) for p in params],
        out_specs=x_spec,
        scratch_shapes=scratch_shapes,
        compiler_params=pltpu.CompilerParams(
            dimension_semantics=("arbitrary", "arbitrary"),
            vmem_limit_bytes=VMEM_LIMIT_BYTES),
        name=name,
    )(x, *params)


def _even_layer(x, norm_pre, w_in, a_conv, b_conv, b_bias, ln_g, ln_b, w_out, norm_post):
    t = SEQ_TILE
    params = (norm_pre[None, :], w_in.astype(_BF16), a_conv, b_conv, b_bias[None, :],
              ln_g[None, :], ln_b[None, :], w_out.astype(_BF16), norm_post[None, :])
    scratch = [pltpu.VMEM((HALO + t, W_BRANCH), _F32),
               pltpu.VMEM((HALO + t, W_BRANCH), _F32),
               pltpu.VMEM((t, 2 * W_BRANCH), _BF16)]
    return _layer_call(_even_layer_kernel, x, params, scratch, "even_layer")


def _odd_layer(x, norm_pre, w_in, c_w, c_b, c_scale, w_out, norm_post):
    t = SEQ_TILE
    params = (norm_pre[None, :], w_in.astype(_BF16), c_w.astype(_BF16),
              c_b.reshape(1, D_MODEL), c_scale[None, :], w_out.astype(_BF16),
              norm_post[None, :])
    scratch = [pltpu.VMEM((HALO + t, D_MODEL), _F32),
               pltpu.VMEM((t, D_MODEL), _BF16)]
    return _layer_call(_odd_layer_kernel, x, params, scratch, "odd_layer")


def kernel(x, e_norm_pre, e_norm_post, e_w_in, e_a_conv, e_b_conv, e_b_conv_bias, e_b_ln_g, e_b_ln_b, e_w_out, o_norm_pre, o_norm_post, o_w_in, o_c_w, o_c_b, o_c_scale, o_w_out):
    depth = e_norm_pre.shape[0] + o_norm_pre.shape[0]
    for layer in range(depth):
        i = layer // 2
        if layer % 2 == 0:
            x = _even_layer(x, e_norm_pre[i], e_w_in[i], e_a_conv[i], e_b_conv[i],
                            e_b_conv_bias[i], e_b_ln_g[i], e_b_ln_b[i], e_w_out[i],
                            e_norm_post[i])
        else:
            x = _odd_layer(x, o_norm_pre[i], o_w_in[i], o_c_w[i], o_c_b[i], o_c_scale[i],
                           o_w_out[i], o_norm_post[i])
    return x
```
